```python
import math, functools
import jax, jax.numpy as jnp
from jax import lax
import numpy as np

D_MODEL = 2048
BATCH = 1
SEQ = 8192
DEPTH = 2
DEC_BATCH = 32
DEC_SEQ = 1
PAST_LEN = 8192
PAGE_SIZE = 128

ATT_HEADS = 8
QK_DIM = 64
V_DIM = 2 * QK_DIM
ATT_WIDTH = ATT_HEADS * V_DIM
QK_WIDTH = ATT_HEADS * 2 * QK_DIM
ROT_DIM = QK_DIM // 4
ROPE_THETA = 500000.0
Q_BLOCK = 128
D_RNN = 1024
RNN_BLOCKS = 8
RNN_BS = D_RNN // RNN_BLOCKS
CONV_W = 4
RG_C = 8.0
PLE_DIM = 256
EPS = 1e-6
NEG = -1e30
_SIZES = [QK_WIDTH, QK_WIDTH, ATT_WIDTH, ATT_WIDTH, D_RNN, D_RNN, 2 * D_MODEL]
N_IN = sum(_SIZES)
SPLIT_POINTS = [int(v) for v in np.cumsum(_SIZES)[:-1]]

kernel_name = "hybrid_diffattn_rglru_step"


def rms_norm(x, g):
    xf = x.astype(jnp.float32)
    y = xf * lax.rsqrt(jnp.mean(xf * xf, axis=-1, keepdims=True) + EPS)
    return (y * g.astype(jnp.float32)).astype(x.dtype)


def rope_partial(t, pos):
    half = ROT_DIM // 2
    inv = ROPE_THETA ** (-jnp.arange(half, dtype=jnp.float32) * 2.0 / ROT_DIM)
    ang = pos.astype(jnp.float32)[:, None] * inv[None, :]
    cos = jnp.cos(ang)[:, None, None, :]
    sin = jnp.sin(ang)[:, None, None, :]
    tr = t[..., :ROT_DIM].astype(jnp.float32)
    x1, x2 = tr[..., :half], tr[..., half:]
    rot = jnp.concatenate([x1 * cos - x2 * sin, x2 * cos + x1 * sin], axis=-1)
    return jnp.concatenate([rot.astype(t.dtype), t[..., ROT_DIM:]], axis=-1)


def diff_attn_prompt(q, k, v, lam):
    B, S = q.shape[0], q.shape[1]
    nb = S // Q_BLOCK
    qb = q.reshape(B, nb, Q_BLOCK, ATT_HEADS, 2, QK_DIM).swapaxes(0, 1)
    kpos = jnp.arange(S)
    scale = QK_DIM ** -0.5

    def block(args):
        qi, bi = args
        s = jnp.einsum('bqhmd,bkhmd->bhmqk', qi, k).astype(jnp.float32) * scale
        qpos = bi * Q_BLOCK + jnp.arange(Q_BLOCK)
        mask = kpos[None, :] <= qpos[:, None]
        pr = jax.nn.softmax(jnp.where(mask, s, NEG), axis=-1)
        pd = pr[:, :, 0] - lam * pr[:, :, 1]
        return jnp.einsum('bhqk,bkhd->bqhd', pd.astype(v.dtype), v)

    o = lax.map(block, (qb, jnp.arange(nb)))
    return o.swapaxes(0, 1).reshape(B, S, ATT_HEADS, V_DIM)


def diff_attn_sample(q, k, v, lam, k_past, v_past):
    T = q.shape[1]
    P = k_past.shape[1]
    scale = QK_DIM ** -0.5
    s_past = jnp.einsum('bqhmd,bkhmd->bhmqk', q, k_past).astype(jnp.float32) * scale
    s_new = jnp.einsum('bqhmd,bkhmd->bhmqk', q, k).astype(jnp.float32) * scale
    tpos = jnp.arange(T)
    s_new = jnp.where(tpos[None, :] <= tpos[:, None], s_new, NEG)
    pr = jax.nn.softmax(jnp.concatenate([s_past, s_new], axis=-1), axis=-1)
    pd = (pr[:, :, 0] - lam * pr[:, :, 1]).astype(v.dtype)
    return (jnp.einsum('bhqk,bkhd->bqhd', pd[..., :P], v_past)
            + jnp.einsum('bhqk,bkhd->bqhd', pd[..., P:], v))


def causal_conv(xr, buf, w, b):
    T = xr.shape[1]
    xp = jnp.concatenate([buf.astype(xr.dtype), xr], axis=1)
    y = b + sum(xp[:, j:j + T] * w[j] for j in range(CONV_W))
    return y, xp[:, -(CONV_W - 1):]


def block_diag(x, w, b):
    xb = x.reshape(x.shape[:-1] + (RNN_BLOCKS, RNN_BS))
    return jnp.einsum('btnc,ncd->btnd', xb, w).reshape(x.shape) + b


def rglru(xc, h0, wa, ba, wx, bx, lam):
    r = jax.nn.sigmoid(block_diag(xc, wa, ba)).astype(jnp.float32)
    i = jax.nn.sigmoid(block_diag(xc, wx, bx)).astype(jnp.float32)
    log_a = -RG_C * r * jax.nn.softplus(-lam.astype(jnp.float32))
    a = jnp.exp(log_a)
    u = jnp.sqrt(-jnp.expm1(2.0 * log_a)) * (i * xc.astype(jnp.float32))

    def step(h, au):
        a_t, u_t = au
        h = a_t * h + u_t
        return h, h

    hT, hs = lax.scan(step, h0.astype(jnp.float32), (a.swapaxes(0, 1), u.swapaxes(0, 1)))
    return hs.swapaxes(0, 1).astype(xc.dtype), hT.astype(xc.dtype)


def mixer_layer(x, pe, pos, h0, conv0, attend, lw, lam_init):
    B, T = x.shape[0], x.shape[1]
    xn = rms_norm(x, lw['norm_pre'])
    z = xn @ lw['w_in']
    q, k, v, g_att, x_rnn, g_rnn, g_merge = jnp.split(z, SPLIT_POINTS, axis=-1)
    q = rope_partial(q.reshape(B, T, ATT_HEADS, 2, QK_DIM), pos)
    k = rope_partial(k.reshape(B, T, ATT_HEADS, 2, QK_DIM), pos)
    v = v.reshape(B, T, ATT_HEADS, V_DIM)
    lam = (jnp.exp(jnp.sum(lw['lam_q1'].astype(jnp.float32) * lw['lam_k1'].astype(jnp.float32)))
           - jnp.exp(jnp.sum(lw['lam_q2'].astype(jnp.float32) * lw['lam_k2'].astype(jnp.float32)))
           + lam_init)
    o = attend(q, k, v, lam)
    o = rms_norm(o, lw['subln_g']) * (1.0 - lam_init)
    y_att = (o.reshape(B, T, ATT_WIDTH) * jax.nn.silu(g_att)) @ lw['w_br_attn']
    xc, conv_new = causal_conv(x_rnn, conv0, lw['conv_w'], lw['conv_b'])
    hs, hT = rglru(xc, h0, lw['gate_a_w'], lw['gate_a_b'], lw['gate_x_w'], lw['gate_x_b'], lw['rg_lambda'])
    y_rnn = (hs * jax.nn.silu(g_rnn)) @ lw['w_br_rnn']
    g_a, g_b = jnp.split(g_merge, 2, axis=-1)
    merged = jax.nn.sigmoid(g_a) * y_att + jax.nn.sigmoid(g_b) * y_rnn
    x = x + rms_norm(merged @ lw['w_out'], lw['norm_post'])
    x = x + jax.nn.sigmoid(x @ lw['w_pg']) * (pe @ lw['w_pe'])
    return x, k, v, hT, conv_new


def setup_inputs(seed: int = 0) -> dict:
    key = jax.random.key(seed)
    ks = iter(jax.random.split(key, 40))
    n_pages = PAST_LEN // PAGE_SIZE
    n_used = DEC_BATCH * n_pages
    n_pool = n_used + max(1, n_used // 4)

    def nrm(shape, s):
        return jax.random.normal(next(ks), shape, jnp.float32) * s

    u = jax.random.uniform(next(ks), (DEPTH, D_RNN), jnp.float32, 0.9, 0.999)
    sg = u ** (1.0 / RG_C)
    rg_lambda = jnp.log(sg) - jnp.log1p(-sg)
    page_table = jax.random.permutation(next(ks), n_pool)[:n_used].reshape(DEC_BATCH, n_pages).astype(jnp.int32)
    return {
        'x_prompt': nrm((BATCH, SEQ, D_MODEL), 1.0),
        'x_sample': nrm((DEC_BATCH, DEC_SEQ, D_MODEL), 1.0),
        'cache_k': nrm((DEPTH, n_pool, PAGE_SIZE, ATT_HEADS, 2, QK_DIM), 1.0),
        'cache_v': nrm((DEPTH, n_pool, PAGE_SIZE, ATT_HEADS, V_DIM), 1.0),
        'state_h': nrm((DEPTH, DEC_BATCH, D_RNN), 0.5),
        'state_conv': nrm((DEPTH, DEC_BATCH, CONV_W - 1, D_RNN), 1.0),
        'page_table': page_table,
        'p_prompt': nrm((DEPTH, BATCH, SEQ, PLE_DIM), 1.0),
        'p_sample': nrm((DEPTH, DEC_BATCH, DEC_SEQ, PLE_DIM), 1.0),
        'norm_pre': 1.0 + nrm((DEPTH, D_MODEL), 0.02),
        'w_in': nrm((DEPTH, D_MODEL, N_IN), D_MODEL ** -0.5),
        'lam_q1': nrm((DEPTH, QK_DIM), 0.1),
        'lam_k1': nrm((DEPTH, QK_DIM), 0.1),
        'lam_q2': nrm((DEPTH, QK_DIM), 0.1),
        'lam_k2': nrm((DEPTH, QK_DIM), 0.1),
        'subln_g': 1.0 + nrm((DEPTH, V_DIM), 0.02),
        'w_br_attn': nrm((DEPTH, ATT_WIDTH, D_MODEL), ATT_WIDTH ** -0.5),
        'conv_w': nrm((DEPTH, CONV_W, D_RNN), CONV_W ** -0.5),
        'conv_b': nrm((DEPTH, D_RNN), 0.01),
        'gate_a_w': nrm((DEPTH, RNN_BLOCKS, RNN_BS, RNN_BS), RNN_BS ** -0.5),
        'gate_a_b': nrm((DEPTH, D_RNN), 0.01),
        'gate_x_w': nrm((DEPTH, RNN_BLOCKS, RNN_BS, RNN_BS), RNN_BS ** -0.5),
        'gate_x_b': nrm((DEPTH, D_RNN), 0.01),
        'rg_lambda': rg_lambda,
        'w_br_rnn': nrm((DEPTH, D_RNN, D_MODEL), D_RNN ** -0.5),
        'w_out': nrm((DEPTH, D_MODEL, D_MODEL), D_MODEL ** -0.5),
        'norm_post': 1.0 + nrm((DEPTH, D_MODEL), 0.02),
        'w_pe': nrm((DEPTH, PLE_DIM, D_MODEL), PLE_DIM ** -0.5),
        'w_pg': nrm((DEPTH, D_MODEL, D_MODEL), D_MODEL ** -0.5),
    }


def reference(x_prompt, x_sample, cache_k, cache_v, state_h, state_conv, page_table,
              p_prompt, p_sample, norm_pre, w_in, lam_q1, lam_k1, lam_q2, lam_k2, subln_g,
              w_br_attn, conv_w, conv_b, gate_a_w, gate_a_b, gate_x_w, gate_x_b, rg_lambda,
              w_br_rnn, w_out, norm_post, w_pe, w_pg):
    S = x_prompt.shape[1]
    DB, T = x_sample.shape[0], x_sample.shape[1]
    past = page_table.shape[1] * cache_k.shape[2]
    pos_prompt = jnp.arange(S)
    pos_sample = past + jnp.arange(T)
    xp, xs = x_prompt, x_sample
    kp_l, vp_l, hp_l, cp_l, ks_l, vs_l, hs_l, cs_l = [], [], [], [], [], [], [], []
    for l in range(DEPTH):
        lam_init = 0.8 - 0.6 * math.exp(-0.3 * l)
        lw = dict(norm_pre=norm_pre[l], w_in=w_in[l], lam_q1=lam_q1[l], lam_k1=lam_k1[l],
                  lam_q2=lam_q2[l], lam_k2=lam_k2[l], subln_g=subln_g[l], w_br_attn=w_br_attn[l],
                  conv_w=conv_w[l], conv_b=conv_b[l], gate_a_w=gate_a_w[l], gate_a_b=gate_a_b[l],
                  gate_x_w=gate_x_w[l], gate_x_b=gate_x_b[l], rg_lambda=rg_lambda[l],
                  w_br_rnn=w_br_rnn[l], w_out=w_out[l], norm_post=norm_post[l],
                  w_pe=w_pe[l], w_pg=w_pg[l])
        h0 = jnp.zeros((xp.shape[0], D_RNN), xp.dtype)
        c0 = jnp.zeros((xp.shape[0], CONV_W - 1, D_RNN), xp.dtype)
        xp, kp, vp, hp, cp = mixer_layer(xp, p_prompt[l], pos_prompt, h0, c0,
                                         diff_attn_prompt, lw, lam_init)
        k_past = cache_k[l, page_table].reshape(DB, past, ATT_HEADS, 2, QK_DIM)
        v_past = cache_v[l, page_table].reshape(DB, past, ATT_HEADS, V_DIM)
        attend = functools.partial(diff_attn_sample, k_past=k_past, v_past=v_past)
        xs, ks_, vs_, hs_, cs_ = mixer_layer(xs, p_sample[l], pos_sample, state_h[l], state_conv[l],
                                             attend, lw, lam_init)
        kp_l.append(kp); vp_l.append(vp); hp_l.append(hp); cp_l.append(cp)
        ks_l.append(ks_); vs_l.append(vs_); hs_l.append(hs_); cs_l.append(cs_)
    return (xp, xs,
            jnp.stack(kp_l), jnp.stack(vp_l), jnp.stack(hp_l), jnp.stack(cp_l),
            jnp.stack(ks_l), jnp.stack(vs_l), jnp.stack(hs_l), jnp.stack(cs_l))
```

```python
import functools
import math

import jax
import jax.numpy as jnp
from jax import lax
from jax.experimental import pallas as pl
from jax.experimental.pallas import tpu as pltpu

ATT_HEADS = 8
QK_DIM = 64
V_DIM = 2 * QK_DIM
HEAD_W = 2 * QK_DIM
ATT_WIDTH = ATT_HEADS * V_DIM
ROT_DIM = QK_DIM // 4
ROPE_THETA = 500000.0
RNN_BLOCKS = 8
CONV_W = 4
RG_C = 8.0
EPS = 1e-6
NEG = -1e30
LANES = 128
VMEM_LIMIT = 56 * 1024 * 1024

F32 = jnp.float32
BF16 = jnp.bfloat16


def _params(sem):
    return pltpu.CompilerParams(dimension_semantics=sem, vmem_limit_bytes=VMEM_LIMIT)


def _dot(a, b):
    return jnp.dot(a, b, preferred_element_type=F32)


def _lam(lamv, lam_init):
    a = jnp.sum(lamv[0:1, :] * lamv[1:2, :], axis=1, keepdims=True)
    b = jnp.sum(lamv[2:3, :] * lamv[3:4, :], axis=1, keepdims=True)
    return jnp.exp(a) - jnp.exp(b) + lam_init


def _silu(x):
    return x * jax.nn.sigmoid(x)


def _rope_tables(pos):
    half = ROT_DIM // 2
    inv = ROPE_THETA ** (-jnp.arange(half, dtype=F32) * 2.0 / ROT_DIM)
    ang = pos.astype(F32)[:, None] * inv[None, :]
    cos, sin = jnp.cos(ang), jnp.sin(ang)
    t = pos.shape[0]
    pad = jnp.zeros((t, QK_DIM - ROT_DIM), F32)
    cos_m = jnp.concatenate([cos, cos, pad + 1.0], axis=1)
    s_lo = jnp.concatenate([-sin, jnp.zeros_like(sin), pad], axis=1)
    s_hi = jnp.concatenate([jnp.zeros_like(sin), sin, pad], axis=1)
    rep = LANES // QK_DIM
    return tuple(jnp.tile(a, (1, rep)) for a in (cos_m, s_lo, s_hi))


def _rope(t, cos, s_lo, s_hi):
    half = ROT_DIM // 2
    out = []
    for c in range(t.shape[1] // LANES):
        tc = t[:, c * LANES:(c + 1) * LANES]
        up = pltpu.roll(tc, LANES - half, axis=1)
        dn = pltpu.roll(tc, half, axis=1)
        out.append(tc * cos + up * s_lo + dn * s_hi)
    return jnp.concatenate(out, axis=1)


def _in_proj_kernel(x_ref, g_ref, w_ref, cos_ref, slo_ref, shi_ref, *rest, transposed, tb):
    if transposed:
        q_ref, k_ref, kb_ref, v_ref, vt_ref, zr_ref, xn_ref = rest
    else:
        q_ref, k_ref, kb_ref, v_ref, zr_ref, xn_ref = rest
    j = pl.program_id(1)

    @pl.when(j == 0)
    def _():
        x = x_ref[...]
        y = x * lax.rsqrt(jnp.mean(x * x, axis=-1, keepdims=True) + EPS)
        xn_ref[...] = (y * g_ref[...]).astype(BF16)

    z = _dot(xn_ref[...], w_ref[...])

    @pl.when(j == 0)
    def _():
        q = _rope(z, cos_ref[...], slo_ref[...], shi_ref[...]) * (QK_DIM ** -0.5)
        q_ref[...] = (q.T if transposed else q).astype(BF16)

    @pl.when(j == 1)
    def _():
        k = _rope(z, cos_ref[...], slo_ref[...], shi_ref[...])
        k_ref[...] = k
        kb_ref[...] = k.astype(BF16)

    @pl.when(j == 2)
    def _():
        v_ref[...] = z
        if transposed:
            zt = z.T.astype(BF16)
            for c in range(z.shape[0] // tb):
                vt_ref[c] = zt[:, c * tb:(c + 1) * tb]

    @pl.when(j >= 3)
    def _():
        zr_ref[...] = z


def _in_proj(x, g, w_bf, tables, tm, tb=None):
    m, d = x.shape
    n_in = w_bf.shape[1]
    tn = ATT_WIDTH
    nj = n_in // tn
    cos, slo, shi = tables
    transposed = tb is not None
    row = lambda i, j: (i, 0)
    out_specs = [
        pl.BlockSpec((tn, tm), lambda i, j: (0, i)) if transposed else pl.BlockSpec((tm, tn), row),
        pl.BlockSpec((tm, tn), row),
        pl.BlockSpec((tm, tn), row),
        pl.BlockSpec((tm, tn), row),
    ]
    out_shape = [
        jax.ShapeDtypeStruct((tn, m) if transposed else (m, tn), BF16),
        jax.ShapeDtypeStruct((m, tn), F32),
        jax.ShapeDtypeStruct((m, tn), BF16),
        jax.ShapeDtypeStruct((m, tn), F32),
    ]
    if transposed:
        out_specs.append(pl.BlockSpec((tm // tb, tn, tb), lambda i, j: (i, 0, 0)))
        out_shape.append(jax.ShapeDtypeStruct((m // tb, tn, tb), BF16))
    out_specs.append(pl.BlockSpec((tm, tn), lambda i, j: (i, jnp.maximum(j - 3, 0))))
    out_shape.append(jax.ShapeDtypeStruct((m, n_in - 3 * tn), F32))
    kern = functools.partial(_in_proj_kernel, transposed=transposed, tb=tb)
    return pl.pallas_call(
        kern,
        grid=(m // tm, nj),
        in_specs=[
            pl.BlockSpec((tm, d), row),
            pl.BlockSpec((1, d), lambda i, j: (0, 0)),
            pl.BlockSpec((d, tn), lambda i, j: (0, j)),
            pl.BlockSpec((tm, LANES), row),
            pl.BlockSpec((tm, LANES), row),
            pl.BlockSpec((tm, LANES), row),
        ],
        out_specs=out_specs,
        out_shape=out_shape,
        scratch_shapes=[pltpu.VMEM((tm, d), BF16)],
        compiler_params=_params(("parallel", "arbitrary")),
        name="in_proj",
    )(x, g, w_bf, cos, slo, shi)


def _head_finish(o, gsub, lam_init):
    ms = jnp.mean(o * o, axis=0, keepdims=True)
    return o * lax.rsqrt(ms + EPS) * gsub * (1.0 - lam_init)


def _attn_prompt_kernel(lam_ref, gsub_ref, qt_ref, kb_ref, vt_ref, gate_ref, o_ref,
                        *, tq, tk, lam_init):
    qi = pl.program_id(1)
    lam = _lam(lam_ref[...], lam_init)
    q1 = qt_ref[0:QK_DIM, :]
    q2 = qt_ref[QK_DIM:HEAD_W, :]

    def step(kj, carry, masked):
        m1, l1, a1, m2, l2, a2 = carry
        start = pl.multiple_of(kj * tk, tk)
        kblk = kb_ref[pl.ds(start, tk), :]
        vblk = vt_ref[kj]
        s1 = _dot(kblk[:, 0:QK_DIM], q1)
        s2 = _dot(kblk[:, QK_DIM:HEAD_W], q2)
        if masked:
            kpos = start + lax.broadcasted_iota(jnp.int32, (tk, tq), 0)
            qpos = qi * tq + lax.broadcasted_iota(jnp.int32, (tk, tq), 1)
            keep = kpos <= qpos
            s1 = jnp.where(keep, s1, NEG)
            s2 = jnp.where(keep, s2, NEG)

        def upd(s, m, l, a):
            m_new = jnp.maximum(m, jnp.max(s, axis=0, keepdims=True))
            alpha = jnp.exp(m - m_new)
            p = jnp.exp(s - m_new)
            l = alpha * l + jnp.sum(p, axis=0, keepdims=True)
            a = alpha * a + _dot(vblk, p.astype(BF16))
            return m_new, l, a

        m1, l1, a1 = upd(s1, m1, l1, a1)
        m2, l2, a2 = upd(s2, m2, l2, a2)
        return m1, l1, a1, m2, l2, a2

    row = jnp.full((1, tq), NEG, F32)
    zero = jnp.zeros((1, tq), F32)
    acc = jnp.zeros((V_DIM, tq), F32)
    carry = (row, zero, acc, row, zero, acc)
    n_full = (qi * tq) // tk
    carry = lax.fori_loop(0, n_full, lambda kj, c: step(kj, c, False), carry)
    for d in range(max(tq // tk, 1)):
        carry = step(n_full + d, carry, True)
    m1, l1, a1, m2, l2, a2 = carry
    o = a1 / l1 - lam * (a2 / l2)
    y = _head_finish(o, gsub_ref[...], lam_init)
    o_ref[...] = (y.T * _silu(gate_ref[...])).astype(o_ref.dtype)


def _attn_prompt(lamv, gsub, qt, kb, vt, zr, lam_init, tq, tk):
    s = kb.shape[0]
    kern = functools.partial(_attn_prompt_kernel, tq=tq, tk=tk, lam_init=lam_init)
    return pl.pallas_call(
        kern,
        grid=(ATT_HEADS, s // tq),
        in_specs=[
            pl.BlockSpec(lamv.shape, lambda h, i: (0, 0)),
            pl.BlockSpec((V_DIM, 1), lambda h, i: (0, 0)),
            pl.BlockSpec((HEAD_W, tq), lambda h, i: (h, i)),
            pl.BlockSpec((s, HEAD_W), lambda h, i: (0, h)),
            pl.BlockSpec((s // tk, V_DIM, tk), lambda h, i: (0, h, 0)),
            pl.BlockSpec((tq, V_DIM), lambda h, i: (i, h)),
        ],
        out_specs=pl.BlockSpec((tq, V_DIM), lambda h, i: (i, h)),
        out_shape=jax.ShapeDtypeStruct((s, ATT_WIDTH), BF16),
        compiler_params=_params(("parallel", "parallel")),
        name="attn_prompt",
    )(lamv, gsub, qt, kb, vt, zr)


def _attn_sample_kernel(pt_ref, lam_ref, gsub_ref, q_ref, kn_ref, vn_ref, gate_ref, *rest,
                        pps, lam_init):
    k_refs = rest[0:pps]
    v_refs = rest[pps:2 * pps]
    o_ref = rest[2 * pps]
    m_ref, l_ref, acc_ref = rest[2 * pps + 1:]
    g = pl.program_id(1)
    nmap = 2 * ATT_HEADS
    page = k_refs[0].shape[0]

    colh = lax.broadcasted_iota(jnp.int32, (nmap, ATT_WIDTH), 1)
    rowi = lax.broadcasted_iota(jnp.int32, (nmap, ATT_WIDTH), 0)
    sel = (colh // QK_DIM) == (2 * (rowi % ATT_HEADS) + rowi // ATT_HEADS)
    diag = (colh // V_DIM) == (rowi % ATT_HEADS)
    qrow = q_ref[...].astype(F32)
    qblk = jnp.where(sel, qrow, 0.0)

    @pl.when(g == 0)
    def _():
        s_new = jnp.sum(qblk * kn_ref[...].astype(BF16).astype(F32), axis=1, keepdims=True)
        m_ref[...] = s_new
        l_ref[...] = jnp.ones_like(s_new)
        acc_ref[...] = jnp.broadcast_to(vn_ref[...], acc_ref.shape)

    qpad = jnp.concatenate(
        [qblk, jnp.zeros((LANES - nmap, ATT_WIDTH), F32)], axis=0).astype(BF16)
    st = []
    for j in range(pps):
        kp = k_refs[j][...].astype(BF16)
        sj = lax.dot_general(kp, qpad, (((1,), (1,)), ((), ())),
                             preferred_element_type=F32)
        st.append(sj.T[0:nmap, :])
    s = jnp.concatenate(st, axis=1)
    m_old = m_ref[...]
    m_new = jnp.maximum(m_old, jnp.max(s, axis=1, keepdims=True))
    alpha = jnp.exp(m_old - m_new)
    p = jnp.exp(s - m_new)
    l_ref[...] = alpha * l_ref[...] + jnp.sum(p, axis=1, keepdims=True)
    m_ref[...] = m_new
    pb = p.astype(BF16)
    pv = jnp.zeros(acc_ref.shape, F32)
    for j in range(pps):
        pv = pv + _dot(pb[:, j * page:(j + 1) * page], v_refs[j][...].astype(BF16))
    acc_ref[...] = alpha * acc_ref[...] + pv

    @pl.when(g == pl.num_programs(1) - 1)
    def _():
        lam = _lam(lam_ref[...], lam_init)
        on = jnp.where(diag, acc_ref[...] / l_ref[...], 0.0)
        y = on[0:ATT_HEADS, :] - lam * on[ATT_HEADS:nmap, :]
        ms = jnp.sum(y * y, axis=1, keepdims=True) / V_DIM
        yn = y * lax.rsqrt(ms + EPS)
        orow = jnp.sum(yn, axis=0, keepdims=True)
        orow = orow * gsub_ref[...] * (1.0 - lam_init)
        o_ref[...] = (orow * _silu(gate_ref[...])).astype(o_ref.dtype)


def _attn_sample(page_table, lamv, gsub_row, q, kn, vn, gate, ck, cv, layer, lam_init, pps):
    db, n_pages = page_table.shape
    page = ck.shape[2]
    kern = functools.partial(_attn_sample_kernel, pps=pps, lam_init=lam_init)
    rowspec = pl.BlockSpec((None, 1, ATT_WIDTH), lambda b, g, pt: (b, 0, 0))

    def page_spec(j):
        return pl.BlockSpec((None, None, page, ATT_WIDTH),
                            lambda b, g, pt: (layer, pt[b, g * pps + j], 0, 0))

    grid_spec = pltpu.PrefetchScalarGridSpec(
        num_scalar_prefetch=1,
        grid=(db, n_pages // pps),
        in_specs=[
            pl.BlockSpec(lamv.shape, lambda b, g, pt: (0, 0)),
            pl.BlockSpec((1, ATT_WIDTH), lambda b, g, pt: (0, 0)),
            rowspec, rowspec, rowspec, rowspec,
        ] + [page_spec(j) for j in range(pps)] * 2,
        out_specs=rowspec,
        scratch_shapes=[
            pltpu.VMEM((2 * ATT_HEADS, 1), F32),
            pltpu.VMEM((2 * ATT_HEADS, 1), F32),
            pltpu.VMEM((2 * ATT_HEADS, ATT_WIDTH), F32),
        ],
    )
    return pl.pallas_call(
        kern,
        grid_spec=grid_spec,
        out_shape=jax.ShapeDtypeStruct((db, 1, ATT_WIDTH), BF16),
        compiler_params=_params(("parallel", "arbitrary")),
        name="attn_sample",
    )(page_table, lamv, gsub_row, q, kn, vn, gate, *([ck] * pps), *([cv] * pps))


def _rg_gates(xc, wg_ref, ba, bx, lam):
    bs = xc.shape[1] // RNN_BLOCKS
    ra, ia = [], []
    for n in range(RNN_BLOCKS):
        y = _dot(xc[:, n * bs:(n + 1) * bs].astype(BF16), wg_ref[n])
        ra.append(y[:, 0:bs])
        ia.append(y[:, bs:2 * bs])
    r = jax.nn.sigmoid(jnp.concatenate(ra, axis=1) + ba)
    i = jax.nn.sigmoid(jnp.concatenate(ia, axis=1) + bx)
    log_a = -RG_C * r * jax.nn.softplus(-lam)
    a = jnp.exp(log_a)
    u = jnp.sqrt(-jnp.tanh(log_a) * (a * a + 1.0)) * (i * xc)
    return a, u


def _rglru_prompt_kernel(x_ref, gate_ref, cw_ref, cb_ref, wg_ref, ba_ref, bx_ref, lam_ref,
                         o_ref, ht_ref, cn_ref, xp_ref, h_ref, *, tm):
    i = pl.program_id(0)
    pad = 8

    @pl.when(i == 0)
    def _():
        xp_ref[0:pad, :] = jnp.zeros((pad, xp_ref.shape[1]), F32)
        h_ref[...] = jnp.zeros_like(h_ref)

    x = x_ref[...]
    xp_ref[pad:pad + tm, :] = x
    cw = cw_ref[...]
    xc = cb_ref[...] + x * cw[CONV_W - 1:CONV_W, :]
    for d in range(1, CONV_W):
        xc = xc + xp_ref[pad - d:pad - d + tm, :] * cw[CONV_W - 1 - d:CONV_W - d, :]
    a, u = _rg_gates(xc, wg_ref, ba_ref[...], bx_ref[...], lam_ref[...])

    rows = lax.broadcasted_iota(jnp.int32, a.shape, 0)
    d = 1
    while d < tm:
        keep = rows >= d
        a_s = jnp.where(keep, pltpu.roll(a, d, axis=0), 1.0)
        u_s = jnp.where(keep, pltpu.roll(u, d, axis=0), 0.0)
        u = u + a * u_s
        a = a * a_s
        d *= 2
    h = a * h_ref[...] + u
    o_ref[...] = (h * _silu(gate_ref[...])).astype(o_ref.dtype)
    h_last = h[tm - 1:tm, :]
    h_ref[...] = h_last
    ht_ref[...] = h_last
    cn_ref[...] = x[tm - (CONV_W - 1):tm, :]
    xp_ref[0:pad, :] = x[tm - pad:tm, :]


def _rglru_prompt(zr, cw, cb, wg, ba, bx, lam, tm):
    s = zr.shape[0]
    dr = cw.shape[1]
    const2 = lambda i: (0, 0)
    kern = functools.partial(_rglru_prompt_kernel, tm=tm)
    return pl.pallas_call(
        kern,
        grid=(s // tm,),
        in_specs=[
            pl.BlockSpec((tm, dr), lambda i: (i, 1)),
            pl.BlockSpec((tm, dr), lambda i: (i, 2)),
            pl.BlockSpec(cw.shape, const2),
            pl.BlockSpec((1, dr), const2),
            pl.BlockSpec(wg.shape, lambda i: (0, 0, 0)),
            pl.BlockSpec((1, dr), const2),
            pl.BlockSpec((1, dr), const2),
            pl.BlockSpec((1, dr), const2),
        ],
        out_specs=[
            pl.BlockSpec((tm, dr), lambda i: (i, 0)),
            pl.BlockSpec((1, dr), const2),
            pl.BlockSpec((CONV_W - 1, dr), const2),
        ],
        out_shape=[
            jax.ShapeDtypeStruct((s, dr), BF16),
            jax.ShapeDtypeStruct((1, dr), F32),
            jax.ShapeDtypeStruct((CONV_W - 1, dr), F32),
        ],
        scratch_shapes=[pltpu.VMEM((tm + 8, dr), F32), pltpu.VMEM((1, dr), F32)],
        compiler_params=_params(("arbitrary",)),
        name="rglru_prompt",
    )(zr, zr, cw, cb, wg, ba, bx, lam)


def _rglru_sample_kernel(x_ref, gate_ref, h0_ref, c0_ref, cw_ref, cb_ref, wg_ref, ba_ref,
                         bx_ref, lam_ref, o_ref, ht_ref, cn_ref):
    x = x_ref[...]
    cw = cw_ref[...]
    xc = cb_ref[...] + x * cw[CONV_W - 1:CONV_W, :]
    for j in range(CONV_W - 1):
        xc = xc + c0_ref[j] * cw[j:j + 1, :]
    a, u = _rg_gates(xc, wg_ref, ba_ref[...], bx_ref[...], lam_ref[...])
    h = a * h0_ref[...] + u
    o_ref[...] = (h * _silu(gate_ref[...])).astype(o_ref.dtype)
    ht_ref[...] = h
    for j in range(CONV_W - 2):
        cn_ref[j] = c0_ref[j + 1]
    cn_ref[CONV_W - 2] = x


def _rglru_sample(zr, h0, c0, cw, cb, wg, ba, bx, lam):
    db = zr.shape[0]
    dr = cw.shape[1]
    const2 = lambda i: (0, 0)
    const3 = lambda i: (0, 0, 0)
    return pl.pallas_call(
        _rglru_sample_kernel,
        grid=(1,),
        in_specs=[
            pl.BlockSpec((db, dr), lambda i: (0, 1)),
            pl.BlockSpec((db, dr), lambda i: (0, 2)),
            pl.BlockSpec((db, dr), const2),
            pl.BlockSpec(c0.shape, const3),
            pl.BlockSpec(cw.shape, const2),
            pl.BlockSpec((1, dr), const2),
            pl.BlockSpec(wg.shape, const3),
            pl.BlockSpec((1, dr), const2),
            pl.BlockSpec((1, dr), const2),
            pl.BlockSpec((1, dr), const2),
        ],
        out_specs=[
            pl.BlockSpec((db, dr), const2),
            pl.BlockSpec((db, dr), const2),
            pl.BlockSpec(c0.shape, const3),
        ],
        out_shape=[
            jax.ShapeDtypeStruct((db, dr), BF16),
            jax.ShapeDtypeStruct((db, dr), F32),
            jax.ShapeDtypeStruct(c0.shape, F32),
        ],
        compiler_params=_params(("arbitrary",)),
        name="rglru_sample",
    )(zr, zr, h0, c0, cw, cb, wg, ba, bx, lam)


def _merge_kernel(aa_ref, ar_ref, ga_ref, gb_ref, wa_ref, wr_ref, o_ref):
    ya = _dot(aa_ref[...], wa_ref[...])
    yr = _dot(ar_ref[...], wr_ref[...])
    o_ref[...] = (jax.nn.sigmoid(ga_ref[...]) * ya
                  + jax.nn.sigmoid(gb_ref[...]) * yr).astype(o_ref.dtype)


def _merge(aa, ar, zr, wa, wr, tm):
    m, ka = aa.shape
    kr = ar.shape[1]
    d = wa.shape[1]
    tn = ATT_WIDTH
    nc = d // tn
    ga0 = 3
    gb0 = 3 + nc
    return pl.pallas_call(
        _merge_kernel,
        grid=(m // tm, nc),
        in_specs=[
            pl.BlockSpec((tm, ka), lambda i, c: (i, 0)),
            pl.BlockSpec((tm, kr), lambda i, c: (i, 0)),
            pl.BlockSpec((tm, tn), lambda i, c: (i, ga0 + c)),
            pl.BlockSpec((tm, tn), lambda i, c: (i, gb0 + c)),
            pl.BlockSpec((ka, tn), lambda i, c: (0, c)),
            pl.BlockSpec((kr, tn), lambda i, c: (0, c)),
        ],
        out_specs=pl.BlockSpec((tm, tn), lambda i, c: (i, c)),
        out_shape=jax.ShapeDtypeStruct((m, d), BF16),
        compiler_params=_params(("parallel", "arbitrary")),
        name="merge",
    )(aa, ar, zr, zr, wa, wr)


def _out_proj_kernel(x_ref, mg_ref, pe_ref, wo_ref, gpost_ref, wpg_ref, wpe_ref, o_ref):
    t = _dot(mg_ref[...], wo_ref[...])
    y = t * lax.rsqrt(jnp.mean(t * t, axis=-1, keepdims=True) + EPS)
    x1 = x_ref[...] + y * gpost_ref[...]
    gate = jax.nn.sigmoid(_dot(x1.astype(BF16), wpg_ref[...]))
    emb = _dot(pe_ref[...].astype(BF16), wpe_ref[...])
    o_ref[...] = x1 + gate * emb


def _out_proj(x, mg, pe, wo, gpost, wpg, wpe, tm):
    m, d = x.shape
    pd = pe.shape[1]
    const2 = lambda i: (0, 0)
    row = lambda i: (i, 0)
    return pl.pallas_call(
        _out_proj_kernel,
        grid=(m // tm,),
        in_specs=[
            pl.BlockSpec((tm, d), row),
            pl.BlockSpec((tm, d), row),
            pl.BlockSpec((tm, pd), row),
            pl.BlockSpec((d, d), const2),
            pl.BlockSpec((1, d), const2),
            pl.BlockSpec((d, d), const2),
            pl.BlockSpec((pd, d), const2),
        ],
        out_specs=pl.BlockSpec((tm, d), row),
        out_shape=jax.ShapeDtypeStruct((m, d), F32),
        compiler_params=_params(("parallel",)),
        name="out_proj",
    )(x, mg, pe, wo, gpost, wpg, wpe)


def _tile(m, cap):
    return m if m <= cap else cap


def kernel(x_prompt, x_sample, cache_k, cache_v, state_h, state_conv, page_table, p_prompt, p_sample, norm_pre, w_in, lam_q1, lam_k1, lam_q2, lam_k2, subln_g, w_br_attn, conv_w, conv_b, gate_a_w, gate_a_b, gate_x_w, gate_x_b, rg_lambda, w_br_rnn, w_out, norm_post, w_pe, w_pg):
    depth = w_in.shape[0]
    _, s, d = x_prompt.shape
    db, t_dec, _ = x_sample.shape
    n_pool, page = cache_k.shape[1], cache_k.shape[2]
    past = page_table.shape[1] * page
    assert x_prompt.shape[0] == 1 and t_dec == 1

    tab_p = _rope_tables(jnp.arange(s))
    tab_s = _rope_tables(jnp.broadcast_to(jnp.asarray(past), (db,)))
    ck = cache_k.reshape(depth, n_pool, page, ATT_WIDTH)
    cv = cache_v.reshape(depth, n_pool, page, ATT_WIDTH)

    xp = x_prompt[0]
    xs = x_sample[:, 0]
    outs = {n: [] for n in ("kp", "vp", "hp", "cp", "ks", "vs", "hs", "cs")}
    tm_p = _tile(s, 512)
    tq = _tile(s, 256)
    pps = math.gcd(page_table.shape[1], 8)

    for l in range(depth):
        lam_init = 0.8 - 0.6 * math.exp(-0.3 * l)
        row = lambda a: a[l].reshape(1, -1)
        w_in_b = w_in[l].astype(BF16)
        wa_b = w_br_attn[l].astype(BF16)
        wr_b = w_br_rnn[l].astype(BF16)
        wo_b = w_out[l].astype(BF16)
        wpg_b = w_pg[l].astype(BF16)
        wpe_b = w_pe[l].astype(BF16)
        wg = jnp.concatenate([gate_a_w[l], gate_x_w[l]], axis=-1).astype(BF16)
        lamv = jnp.stack([lam_q1[l], lam_k1[l], lam_q2[l], lam_k2[l]])
        gsub_col = subln_g[l].reshape(V_DIM, 1)
        gsub_row = jnp.tile(subln_g[l], ATT_HEADS).reshape(1, ATT_WIDTH)
        rg = (conv_w[l], row(conv_b), wg, row(gate_a_b), row(gate_x_b), row(rg_lambda))

        qt, k, kb, v, vt, zr = _in_proj(xp, row(norm_pre), w_in_b, tab_p, tm_p, tb=tq)
        aa = _attn_prompt(lamv, gsub_col, qt, kb, vt, zr, lam_init, tq, tq)
        ar, ht, cn = _rglru_prompt(zr, *rg, tm=tq)
        mg = _merge(aa, ar, zr, wa_b, wr_b, tm_p)
        xp = _out_proj(xp, mg, p_prompt[l, 0], wo_b, row(norm_post), wpg_b, wpe_b, tq)
        outs["kp"].append(k); outs["vp"].append(v); outs["hp"].append(ht); outs["cp"].append(cn)

        q, k, kb, v, zr = _in_proj(xs, row(norm_pre), w_in_b, tab_s, db)
        r3 = lambda a: a.reshape(db, 1, ATT_WIDTH)
        aa = _attn_sample(page_table, lamv, gsub_row, r3(q), r3(k), r3(v),
                          r3(zr[:, 0:ATT_WIDTH]), ck, cv, l, lam_init, pps)
        c0 = jnp.swapaxes(state_conv[l], 0, 1)
        ar, ht, cn = _rglru_sample(zr, state_h[l], c0, *rg)
        mg = _merge(aa.reshape(db, ATT_WIDTH), ar, zr, wa_b, wr_b, db)
        xs = _out_proj(xs, mg, p_sample[l, :, 0], wo_b, row(norm_post), wpg_b, wpe_b, db)
        outs["ks"].append(k); outs["vs"].append(v); outs["hs"].append(ht)
        outs["cs"].append(jnp.swapaxes(cn, 0, 1))

    st = {n: jnp.stack(v) for n, v in outs.items()}
    return (xp[None], xs[:, None],
            st["kp"].reshape(depth, 1, s, ATT_HEADS, 2, QK_DIM),
            st["vp"].reshape(depth, 1, s, ATT_HEADS, V_DIM),
            st["hp"], st["cp"].reshape(depth, 1, CONV_W - 1, -1),
            st["ks"].reshape(depth, db, 1, ATT_HEADS, 2, QK_DIM),
            st["vs"].reshape(depth, db, 1, ATT_HEADS, V_DIM),
            st["hs"], st["cs"])
```

```python
import functools
import math

import jax
import jax.numpy as jnp
from jax import lax
from jax.experimental import pallas as pl
from jax.experimental.pallas import tpu as pltpu

ATT_HEADS = 8
QK_DIM = 64
V_DIM = 2 * QK_DIM
HEAD_W = 2 * QK_DIM
ATT_WIDTH = ATT_HEADS * V_DIM
ROT_DIM = QK_DIM // 4
ROPE_THETA = 500000.0
RNN_BLOCKS = 8
CONV_W = 4
RG_C = 8.0
EPS = 1e-6
NEG = -1e30
LANES = 128
VMEM_LIMIT = 56 * 1024 * 1024

F32 = jnp.float32
BF16 = jnp.bfloat16


def _params(sem):
    return pltpu.CompilerParams(dimension_semantics=sem, vmem_limit_bytes=VMEM_LIMIT)


def _dot(a, b):
    return jnp.dot(a, b, preferred_element_type=F32)


def _lam(lamv, lam_init):
    a = jnp.sum(lamv[0:1, :] * lamv[1:2, :], axis=1, keepdims=True)
    b = jnp.sum(lamv[2:3, :] * lamv[3:4, :], axis=1, keepdims=True)
    return jnp.exp(a) - jnp.exp(b) + lam_init


def _silu(x):
    return x * jax.nn.sigmoid(x)


def _rope_tables(pos):
    half = ROT_DIM // 2
    inv = ROPE_THETA ** (-jnp.arange(half, dtype=F32) * 2.0 / ROT_DIM)
    ang = pos.astype(F32)[:, None] * inv[None, :]
    cos, sin = jnp.cos(ang), jnp.sin(ang)
    t = pos.shape[0]
    pad = jnp.zeros((t, QK_DIM - ROT_DIM), F32)
    cos_m = jnp.concatenate([cos, cos, pad + 1.0], axis=1)
    s_lo = jnp.concatenate([-sin, jnp.zeros_like(sin), pad], axis=1)
    s_hi = jnp.concatenate([jnp.zeros_like(sin), sin, pad], axis=1)
    rep = LANES // QK_DIM
    return tuple(jnp.tile(a, (1, rep)) for a in (cos_m, s_lo, s_hi))


def _rope(t, cos, s_lo, s_hi):
    half = ROT_DIM // 2
    out = []
    for c in range(t.shape[1] // LANES):
        tc = t[:, c * LANES:(c + 1) * LANES]
        up = pltpu.roll(tc, LANES - half, axis=1)
        dn = pltpu.roll(tc, half, axis=1)
        out.append(tc * cos + up * s_lo + dn * s_hi)
    return jnp.concatenate(out, axis=1)


def _in_proj_kernel(x_ref, g_ref, w_ref, cos_ref, slo_ref, shi_ref, *rest, transposed, tb):
    if transposed:
        q_ref, k_ref, kb_ref, v_ref, vt_ref, zr_ref, xn_ref = rest
    else:
        q_ref, k_ref, kb_ref, v_ref, zr_ref, xn_ref = rest
    j = pl.program_id(1)

    @pl.when(j == 0)
    def _():
        x = x_ref[...]
        y = x * lax.rsqrt(jnp.mean(x * x, axis=-1, keepdims=True) + EPS)
        xn_ref[...] = (y * g_ref[...]).astype(BF16)

    z = _dot(xn_ref[...], w_ref[...])

    @pl.when(j == 0)
    def _():
        q = _rope(z, cos_ref[...], slo_ref[...], shi_ref[...]) * (QK_DIM ** -0.5)
        q_ref[...] = (q.T if transposed else q).astype(BF16)

    @pl.when(j == 1)
    def _():
        k = _rope(z, cos_ref[...], slo_ref[...], shi_ref[...])
        k_ref[...] = k.T if transposed else k
        kb_ref[...] = k.astype(BF16)

    @pl.when(j == 2)
    def _():
        v_ref[...] = z
        if transposed:
            zt = z.T.astype(BF16)
            for c in range(z.shape[0] // tb):
                vt_ref[c] = zt[:, c * tb:(c + 1) * tb]

    @pl.when(j >= 3)
    def _():
        zr_ref[...] = z


def _in_proj(x, g, w_bf, tables, tm, tb=None):
    m, d = x.shape
    n_in = w_bf.shape[1]
    tn = ATT_WIDTH
    nj = n_in // tn
    cos, slo, shi = tables
    transposed = tb is not None
    row = lambda i, j: (i, 0)
    out_specs = [
        pl.BlockSpec((tn, tm), lambda i, j: (0, i)) if transposed else pl.BlockSpec((tm, tn), row),
        pl.BlockSpec((tn, tm), lambda i, j: (0, i)) if transposed else pl.BlockSpec((tm, tn), row),
        pl.BlockSpec((tm, tn), row),
        pl.BlockSpec((tm, tn), row),
    ]
    out_shape = [
        jax.ShapeDtypeStruct((tn, m) if transposed else (m, tn), BF16),
        jax.ShapeDtypeStruct((tn, m) if transposed else (m, tn), F32),
        jax.ShapeDtypeStruct((m, tn), BF16),
        jax.ShapeDtypeStruct((m, tn), F32),
    ]
    if transposed:
        out_specs.append(pl.BlockSpec((tm // tb, tn, tb), lambda i, j: (i, 0, 0)))
        out_shape.append(jax.ShapeDtypeStruct((m // tb, tn, tb), BF16))
    out_specs.append(pl.BlockSpec((tm, tn), lambda i, j: (i, jnp.maximum(j - 3, 0))))
    out_shape.append(jax.ShapeDtypeStruct((m, n_in - 3 * tn), F32))
    kern = functools.partial(_in_proj_kernel, transposed=transposed, tb=tb)
    return pl.pallas_call(
        kern,
        grid=(m // tm, nj),
        in_specs=[
            pl.BlockSpec((tm, d), row),
            pl.BlockSpec((1, d), lambda i, j: (0, 0)),
            pl.BlockSpec((d, tn), lambda i, j: (0, j)),
            pl.BlockSpec((tm, LANES), row),
            pl.BlockSpec((tm, LANES), row),
            pl.BlockSpec((tm, LANES), row),
        ],
        out_specs=out_specs,
        out_shape=out_shape,
        scratch_shapes=[pltpu.VMEM((tm, d), BF16)],
        compiler_params=_params(("parallel", "arbitrary")),
        name="in_proj",
    )(x, g, w_bf, cos, slo, shi)


def _head_finish(o, gsub, lam_init):
    ms = jnp.mean(o * o, axis=0, keepdims=True)
    return o * lax.rsqrt(ms + EPS) * gsub * (1.0 - lam_init)


def _attn_prompt_kernel(lam_ref, gsub_ref, qt_ref, kb_ref, vt_ref, gate_ref, o_ref, acc_ref,
                        *, tq, tk, hpb, lam_init):
    qi = pl.program_id(1)
    lam = _lam(lam_ref[...], lam_init)
    nchain = 2 * hpb
    acc_ref[...] = jnp.zeros_like(acc_ref)
    half = lax.broadcasted_iota(jnp.int32, (HEAD_W, tq), 0) < QK_DIM

    def q_map(c):
        hh, mp = divmod(c, 2)
        q = qt_ref[hh * HEAD_W:(hh + 1) * HEAD_W, :]
        return jnp.where(half if mp == 0 else ~half, q, jnp.zeros_like(q))

    def step(kj, carry, masked):
        start = pl.multiple_of(kj * tk, tk)
        kblk = kb_ref[pl.ds(start, tk), :]
        vblk = vt_ref[kj]
        if masked:
            kpos = start + lax.broadcasted_iota(jnp.int32, (tk, tq), 0)
            qpos = qi * tq + lax.broadcasted_iota(jnp.int32, (tk, tq), 1)
            keep = kpos <= qpos
        scores = [_dot(kblk[:, (c // 2) * HEAD_W:(c // 2 + 1) * HEAD_W], q_map(c))
                  for c in range(nchain)]
        out, probs, alphas = [], [], []
        for c in range(nchain):
            m, l = carry[c]
            s = jnp.where(keep, scores[c], NEG) if masked else scores[c]
            m_new = jnp.maximum(m, jnp.max(s, axis=0, keepdims=True))
            alpha = jnp.exp(m - m_new)
            p = jnp.exp(s - m_new)
            out.append((m_new, alpha * l + jnp.sum(p, axis=0, keepdims=True)))
            probs.append(p.astype(BF16))
            alphas.append(alpha)
        for c in range(nchain):
            hh = c // 2
            pv = _dot(vblk[hh * V_DIM:(hh + 1) * V_DIM, :], probs[c])
            acc_ref[c] = alphas[c] * acc_ref[c] + pv
        return tuple(out)

    init = (jnp.full((1, tq), NEG, F32), jnp.zeros((1, tq), F32))
    carry = tuple(init for _ in range(nchain))
    n_full = (qi * tq) // tk
    carry = lax.fori_loop(0, n_full, lambda kj, c: step(kj, c, False), carry)
    for d in range(max(tq // tk, 1)):
        carry = step(n_full + d, carry, True)
    for hh in range(hpb):
        (_, l1), (_, l2) = carry[2 * hh], carry[2 * hh + 1]
        o = acc_ref[2 * hh] / l1 - lam * (acc_ref[2 * hh + 1] / l2)
        y = _head_finish(o, gsub_ref[...], lam_init)
        cols = slice(hh * V_DIM, (hh + 1) * V_DIM)
        o_ref[:, cols] = (y.T * _silu(gate_ref[:, cols])).astype(o_ref.dtype)


def _attn_prompt(lamv, gsub, qt, kb, vt, zr, lam_init, tq, tk, hpb):
    s = kb.shape[0]
    kern = functools.partial(_attn_prompt_kernel, tq=tq, tk=tk, hpb=hpb, lam_init=lam_init)
    return pl.pallas_call(
        kern,
        grid=(ATT_HEADS // hpb, s // tq),
        in_specs=[
            pl.BlockSpec(lamv.shape, lambda h, i: (0, 0)),
            pl.BlockSpec((V_DIM, 1), lambda h, i: (0, 0)),
            pl.BlockSpec((hpb * HEAD_W, tq), lambda h, i: (h, i)),
            pl.BlockSpec((s, hpb * HEAD_W), lambda h, i: (0, h)),
            pl.BlockSpec((s // tk, hpb * V_DIM, tk), lambda h, i: (0, h, 0)),
            pl.BlockSpec((tq, hpb * V_DIM), lambda h, i: (i, h)),
        ],
        out_specs=pl.BlockSpec((tq, hpb * V_DIM), lambda h, i: (i, h)),
        out_shape=jax.ShapeDtypeStruct((s, ATT_WIDTH), BF16),
        scratch_shapes=[pltpu.VMEM((2 * hpb, V_DIM, tq), F32)],
        compiler_params=_params(("parallel", "parallel")),
        name="attn_prompt",
    )(lamv, gsub, qt, kb, vt, zr)


def _attn_sample_kernel(pt_ref, lam_ref, gsub_ref, q_ref, kn_ref, vn_ref, gate_ref, *rest,
                        pps, lam_init):
    k_refs = rest[0:pps]
    v_refs = rest[pps:2 * pps]
    o_ref = rest[2 * pps]
    m_ref, l_ref, acc_ref = rest[2 * pps + 1:]
    g = pl.program_id(1)
    nmap = 2 * ATT_HEADS
    page = k_refs[0].shape[1]

    colh = lax.broadcasted_iota(jnp.int32, (nmap, ATT_WIDTH), 1)
    rowi = lax.broadcasted_iota(jnp.int32, (nmap, ATT_WIDTH), 0)
    sel = (colh // QK_DIM) == (2 * (rowi % ATT_HEADS) + rowi // ATT_HEADS)
    qblk = jnp.where(sel, q_ref[...].astype(F32), 0.0)

    @pl.when(g == 0)
    def _():
        s_new = jnp.sum(qblk * kn_ref[...].astype(BF16).astype(F32), axis=1, keepdims=True)
        m_ref[...] = s_new
        l_ref[...] = jnp.ones_like(s_new)
        acc_ref[...] = jnp.concatenate([vn_ref[...], vn_ref[...]], axis=0)

    qb = qblk.astype(BF16)
    s = jnp.concatenate([_dot(qb, k_refs[j][...].astype(BF16)) for j in range(pps)],
                        axis=1)
    m_old = m_ref[...]
    m_new = jnp.maximum(m_old, jnp.max(s, axis=1, keepdims=True))
    alpha = jnp.exp(m_old - m_new)
    p = jnp.exp(s - m_new)
    l_ref[...] = alpha * l_ref[...] + jnp.sum(p, axis=1, keepdims=True)
    m_ref[...] = m_new

    pst = jnp.concatenate([p[:, j * page:(j + 1) * page] for j in range(pps)], axis=0)
    erow = lax.broadcasted_iota(jnp.int32, (page, page * ATT_HEADS), 0)
    ecol = lax.broadcasted_iota(jnp.int32, (page, page * ATT_HEADS), 1)
    spread = jnp.where(ecol // ATT_HEADS == erow, 1.0, 0.0).astype(BF16)
    pexp = _dot(pst.astype(BF16), spread)
    prow = lax.broadcasted_iota(jnp.int32, pexp.shape, 0)
    pcol = lax.broadcasted_iota(jnp.int32, pexp.shape, 1)
    pexp = jnp.where(pcol % ATT_HEADS == prow % ATT_HEADS, pexp, 0.0).astype(BF16)
    pv = jnp.zeros(acc_ref.shape, F32)
    for j in range(pps):
        pv = pv + _dot(pexp[j * nmap:(j + 1) * nmap, :], v_refs[j][...].astype(BF16))
    acc_ref[...] = alpha * acc_ref[...] + pv

    @pl.when(g == pl.num_programs(1) - 1)
    def _():
        lam = _lam(lam_ref[...], lam_init)
        on = acc_ref[...] / l_ref[...]
        y = on[0:ATT_HEADS, :] - lam * on[ATT_HEADS:nmap, :]
        ms = jnp.mean(y * y, axis=1, keepdims=True)
        yn = y * lax.rsqrt(ms + EPS) * gsub_ref[...] * (1.0 - lam_init)
        o_ref[...] = (yn * _silu(gate_ref[...])).astype(o_ref.dtype)


def _attn_sample(page_table, lamv, gsub_row, q, kn, vn, gate, ckt, cvm, layer, lam_init, pps):
    db, n_pages = page_table.shape
    kern = functools.partial(_attn_sample_kernel, pps=pps, lam_init=lam_init)
    rowspec = pl.BlockSpec((None, 1, ATT_WIDTH), lambda b, g, pt: (b, 0, 0))
    headspec = pl.BlockSpec((None, ATT_HEADS, V_DIM), lambda b, g, pt: (b, 0, 0))

    def page_spec(arr, j):
        return pl.BlockSpec((None, None) + arr.shape[2:],
                            lambda b, g, pt: (layer, pt[b, g * pps + j], 0, 0))

    grid_spec = pltpu.PrefetchScalarGridSpec(
        num_scalar_prefetch=1,
        grid=(db, n_pages // pps),
        in_specs=[
            pl.BlockSpec(lamv.shape, lambda b, g, pt: (0, 0)),
            pl.BlockSpec((1, V_DIM), lambda b, g, pt: (0, 0)),
            rowspec, rowspec, headspec, headspec,
        ] + [page_spec(ckt, j) for j in range(pps)] + [page_spec(cvm, j) for j in range(pps)],
        out_specs=headspec,
        scratch_shapes=[
            pltpu.VMEM((2 * ATT_HEADS, 1), F32),
            pltpu.VMEM((2 * ATT_HEADS, 1), F32),
            pltpu.VMEM((2 * ATT_HEADS, V_DIM), F32),
        ],
    )
    return pl.pallas_call(
        kern,
        grid_spec=grid_spec,
        out_shape=jax.ShapeDtypeStruct((db, ATT_HEADS, V_DIM), BF16),
        compiler_params=_params(("parallel", "arbitrary")),
        name="attn_sample",
    )(page_table, lamv, gsub_row, q, kn, vn, gate, *([ckt] * pps), *([cvm] * pps))


def _rg_gates(xc, wg_ref, ba, bx, lam):
    bs = xc.shape[1] // RNN_BLOCKS
    ra, ia = [], []
    for n in range(RNN_BLOCKS):
        y = _dot(xc[:, n * bs:(n + 1) * bs].astype(BF16), wg_ref[n])
        ra.append(y[:, 0:bs])
        ia.append(y[:, bs:2 * bs])
    r = jax.nn.sigmoid(jnp.concatenate(ra, axis=1) + ba)
    i = jax.nn.sigmoid(jnp.concatenate(ia, axis=1) + bx)
    log_a = -RG_C * r * jax.nn.softplus(-lam)
    a = jnp.exp(log_a)
    u = jnp.sqrt(-jnp.tanh(log_a) * (a * a + 1.0)) * (i * xc)
    return a, u


def _rglru_prompt_kernel(x_ref, gate_ref, cw_ref, cb_ref, wg_ref, ba_ref, bx_ref, lam_ref,
                         o_ref, ht_ref, cn_ref, xp_ref, h_ref, *, tm):
    i = pl.program_id(0)
    pad = 8

    @pl.when(i == 0)
    def _():
        xp_ref[0:pad, :] = jnp.zeros((pad, xp_ref.shape[1]), F32)
        h_ref[...] = jnp.zeros_like(h_ref)

    x = x_ref[...]
    xp_ref[pad:pad + tm, :] = x
    cw = cw_ref[...]
    xc = cb_ref[...] + x * cw[CONV_W - 1:CONV_W, :]
    for d in range(1, CONV_W):
        xc = xc + xp_ref[pad - d:pad - d + tm, :] * cw[CONV_W - 1 - d:CONV_W - d, :]
    a, u = _rg_gates(xc, wg_ref, ba_ref[...], bx_ref[...], lam_ref[...])

    rows = lax.broadcasted_iota(jnp.int32, a.shape, 0)
    d = 1
    while d < tm:
        keep = rows >= d
        a_s = jnp.where(keep, pltpu.roll(a, d, axis=0), 1.0)
        u_s = jnp.where(keep, pltpu.roll(u, d, axis=0), 0.0)
        u = u + a * u_s
        a = a * a_s
        d *= 2
    h = a * h_ref[...] + u
    o_ref[...] = (h * _silu(gate_ref[...])).astype(o_ref.dtype)
    h_last = h[tm - 1:tm, :]
    h_ref[...] = h_last
    ht_ref[...] = h_last
    cn_ref[...] = x[tm - (CONV_W - 1):tm, :]
    xp_ref[0:pad, :] = x[tm - pad:tm, :]


def _rglru_prompt(zr, cw, cb, wg, ba, bx, lam, tm):
    s = zr.shape[0]
    dr = cw.shape[1]
    const2 = lambda i: (0, 0)
    kern = functools.partial(_rglru_prompt_kernel, tm=tm)
    return pl.pallas_call(
        kern,
        grid=(s // tm,),
        in_specs=[
            pl.BlockSpec((tm, dr), lambda i: (i, 1)),
            pl.BlockSpec((tm, dr), lambda i: (i, 2)),
            pl.BlockSpec(cw.shape, const2),
            pl.BlockSpec((1, dr), const2),
            pl.BlockSpec(wg.shape, lambda i: (0, 0, 0)),
            pl.BlockSpec((1, dr), const2),
            pl.BlockSpec((1, dr), const2),
            pl.BlockSpec((1, dr), const2),
        ],
        out_specs=[
            pl.BlockSpec((tm, dr), lambda i: (i, 0)),
            pl.BlockSpec((1, dr), const2),
            pl.BlockSpec((CONV_W - 1, dr), const2),
        ],
        out_shape=[
            jax.ShapeDtypeStruct((s, dr), BF16),
            jax.ShapeDtypeStruct((1, dr), F32),
            jax.ShapeDtypeStruct((CONV_W - 1, dr), F32),
        ],
        scratch_shapes=[pltpu.VMEM((tm + 8, dr), F32), pltpu.VMEM((1, dr), F32)],
        compiler_params=_params(("arbitrary",)),
        name="rglru_prompt",
    )(zr, zr, cw, cb, wg, ba, bx, lam)


def _rglru_sample_kernel(x_ref, gate_ref, h0_ref, c0_ref, cw_ref, cb_ref, wg_ref, ba_ref,
                         bx_ref, lam_ref, o_ref, ht_ref, cn_ref):
    x = x_ref[...]
    cw = cw_ref[...]
    xc = cb_ref[...] + x * cw[CONV_W - 1:CONV_W, :]
    for j in range(CONV_W - 1):
        xc = xc + c0_ref[j] * cw[j:j + 1, :]
    a, u = _rg_gates(xc, wg_ref, ba_ref[...], bx_ref[...], lam_ref[...])
    h = a * h0_ref[...] + u
    o_ref[...] = (h * _silu(gate_ref[...])).astype(o_ref.dtype)
    ht_ref[...] = h
    for j in range(CONV_W - 2):
        cn_ref[j] = c0_ref[j + 1]
    cn_ref[CONV_W - 2] = x


def _rglru_sample(zr, h0, c0, cw, cb, wg, ba, bx, lam):
    db = zr.shape[0]
    dr = cw.shape[1]
    const2 = lambda i: (0, 0)
    const3 = lambda i: (0, 0, 0)
    return pl.pallas_call(
        _rglru_sample_kernel,
        grid=(1,),
        in_specs=[
            pl.BlockSpec((db, dr), lambda i: (0, 1)),
            pl.BlockSpec((db, dr), lambda i: (0, 2)),
            pl.BlockSpec((db, dr), const2),
            pl.BlockSpec(c0.shape, const3),
            pl.BlockSpec(cw.shape, const2),
            pl.BlockSpec((1, dr), const2),
            pl.BlockSpec(wg.shape, const3),
            pl.BlockSpec((1, dr), const2),
            pl.BlockSpec((1, dr), const2),
            pl.BlockSpec((1, dr), const2),
        ],
        out_specs=[
            pl.BlockSpec((db, dr), const2),
            pl.BlockSpec((db, dr), const2),
            pl.BlockSpec(c0.shape, const3),
        ],
        out_shape=[
            jax.ShapeDtypeStruct((db, dr), BF16),
            jax.ShapeDtypeStruct((db, dr), F32),
            jax.ShapeDtypeStruct(c0.shape, F32),
        ],
        compiler_params=_params(("arbitrary",)),
        name="rglru_sample",
    )(zr, zr, h0, c0, cw, cb, wg, ba, bx, lam)


def _merge_kernel(aa_ref, ar_ref, ga_ref, gb_ref, wa_ref, wr_ref, o_ref):
    ya = _dot(aa_ref[...], wa_ref[...])
    yr = _dot(ar_ref[...], wr_ref[...])
    o_ref[...] = (jax.nn.sigmoid(ga_ref[...]) * ya
                  + jax.nn.sigmoid(gb_ref[...]) * yr).astype(o_ref.dtype)


def _merge(aa, ar, zr, wa, wr, tm):
    m, ka = aa.shape
    kr = ar.shape[1]
    d = wa.shape[1]
    tn = ATT_WIDTH
    nc = d // tn
    ga0 = 3
    gb0 = 3 + nc
    return pl.pallas_call(
        _merge_kernel,
        grid=(m // tm, nc),
        in_specs=[
            pl.BlockSpec((tm, ka), lambda i, c: (i, 0)),
            pl.BlockSpec((tm, kr), lambda i, c: (i, 0)),
            pl.BlockSpec((tm, tn), lambda i, c: (i, ga0 + c)),
            pl.BlockSpec((tm, tn), lambda i, c: (i, gb0 + c)),
            pl.BlockSpec((ka, tn), lambda i, c: (0, c)),
            pl.BlockSpec((kr, tn), lambda i, c: (0, c)),
        ],
        out_specs=pl.BlockSpec((tm, tn), lambda i, c: (i, c)),
        out_shape=jax.ShapeDtypeStruct((m, d), BF16),
        compiler_params=_params(("parallel", "arbitrary")),
        name="merge",
    )(aa, ar, zr, zr, wa, wr)


def _out_proj_kernel(x_ref, mg_ref, pe_ref, wo_ref, gpost_ref, wpg_ref, wpe_ref, o_ref):
    t = _dot(mg_ref[...], wo_ref[...])
    y = t * lax.rsqrt(jnp.mean(t * t, axis=-1, keepdims=True) + EPS)
    x1 = x_ref[...] + y * gpost_ref[...]
    gate = jax.nn.sigmoid(_dot(x1.astype(BF16), wpg_ref[...]))
    emb = _dot(pe_ref[...].astype(BF16), wpe_ref[...])
    o_ref[...] = x1 + gate * emb


def _out_proj(x, mg, pe, wo, gpost, wpg, wpe, tm):
    m, d = x.shape
    pd = pe.shape[1]
    const2 = lambda i: (0, 0)
    row = lambda i: (i, 0)
    return pl.pallas_call(
        _out_proj_kernel,
        grid=(m // tm,),
        in_specs=[
            pl.BlockSpec((tm, d), row),
            pl.BlockSpec((tm, d), row),
            pl.BlockSpec((tm, pd), row),
            pl.BlockSpec((d, d), const2),
            pl.BlockSpec((1, d), const2),
            pl.BlockSpec((d, d), const2),
            pl.BlockSpec((pd, d), const2),
        ],
        out_specs=pl.BlockSpec((tm, d), row),
        out_shape=jax.ShapeDtypeStruct((m, d), F32),
        compiler_params=_params(("parallel",)),
        name="out_proj",
    )(x, mg, pe, wo, gpost, wpg, wpe)


def _tile(m, cap):
    return m if m <= cap else cap


def kernel(x_prompt, x_sample, cache_k, cache_v, state_h, state_conv, page_table, p_prompt, p_sample, norm_pre, w_in, lam_q1, lam_k1, lam_q2, lam_k2, subln_g, w_br_attn, conv_w, conv_b, gate_a_w, gate_a_b, gate_x_w, gate_x_b, rg_lambda, w_br_rnn, w_out, norm_post, w_pe, w_pg):
    depth = w_in.shape[0]
    _, s, d = x_prompt.shape
    db, t_dec, _ = x_sample.shape
    n_pool, page = cache_k.shape[1], cache_k.shape[2]
    past = page_table.shape[1] * page
    assert x_prompt.shape[0] == 1 and t_dec == 1

    tab_p = _rope_tables(jnp.arange(s))
    tab_s = _rope_tables(jnp.broadcast_to(jnp.asarray(past), (db,)))
    ckt = jnp.transpose(cache_k, (0, 1, 3, 4, 5, 2)).reshape(depth, n_pool, ATT_WIDTH, page)
    cvm = cache_v.reshape(depth, n_pool, page * ATT_HEADS, V_DIM)

    xp = x_prompt[0]
    xs = x_sample[:, 0]
    outs = {n: [] for n in ("kp", "vp", "hp", "cp", "ks", "vs", "hs", "cs")}
    tm_p = _tile(s, 512)
    tq = _tile(s, 256)
    tk = _tile(s, 512)
    hpb = 4
    pps = math.gcd(page_table.shape[1], 16)

    for l in range(depth):
        lam_init = 0.8 - 0.6 * math.exp(-0.3 * l)
        row = lambda a: a[l].reshape(1, -1)
        w_in_b = w_in[l].astype(BF16)
        wa_b = w_br_attn[l].astype(BF16)
        wr_b = w_br_rnn[l].astype(BF16)
        wo_b = w_out[l].astype(BF16)
        wpg_b = w_pg[l].astype(BF16)
        wpe_b = w_pe[l].astype(BF16)
        wg = jnp.concatenate([gate_a_w[l], gate_x_w[l]], axis=-1).astype(BF16)
        lamv = jnp.stack([lam_q1[l], lam_k1[l], lam_q2[l], lam_k2[l]])
        gsub_col = subln_g[l].reshape(V_DIM, 1)
        gsub_row = subln_g[l].reshape(1, V_DIM)
        rg = (conv_w[l], row(conv_b), wg, row(gate_a_b), row(gate_x_b), row(rg_lambda))

        qt, kt, kb, v, vt, zr = _in_proj(xp, row(norm_pre), w_in_b, tab_p, tm_p, tb=tk)
        aa = _attn_prompt(lamv, gsub_col, qt, kb, vt, zr, lam_init, tq, tk, hpb)
        ar, ht, cn = _rglru_prompt(zr, *rg, tm=tq)
        mg = _merge(aa, ar, zr, wa_b, wr_b, tm_p)
        xp = _out_proj(xp, mg, p_prompt[l, 0], wo_b, row(norm_post), wpg_b, wpe_b, tq)
        outs["kp"].append(kt); outs["vp"].append(v); outs["hp"].append(ht); outs["cp"].append(cn)

        q, k, kb, v, zr = _in_proj(xs, row(norm_pre), w_in_b, tab_s, db)
        r3 = lambda a: a.reshape(db, 1, ATT_WIDTH)
        rh = lambda a: a.reshape(db, ATT_HEADS, V_DIM)
        aa = _attn_sample(page_table, lamv, gsub_row, r3(q), r3(k), rh(v),
                          rh(zr[:, 0:ATT_WIDTH]), ckt, cvm, l, lam_init, pps)
        c0 = jnp.swapaxes(state_conv[l], 0, 1)
        ar, ht, cn = _rglru_sample(zr, state_h[l], c0, *rg)
        mg = _merge(aa.reshape(db, ATT_WIDTH), ar, zr, wa_b, wr_b, db)
        xs = _out_proj(xs, mg, p_sample[l, :, 0], wo_b, row(norm_post), wpg_b, wpe_b, db)
        outs["ks"].append(k); outs["vs"].append(v); outs["hs"].append(ht)
        outs["cs"].append(jnp.swapaxes(cn, 0, 1))

    st = {n: jnp.stack(v) for n, v in outs.items()}
    k_prompt = jnp.transpose(st["kp"].reshape(depth, 1, ATT_HEADS, 2, QK_DIM, s),
                             (0, 1, 5, 2, 3, 4))
    return (xp[None], xs[:, None], k_prompt,
            st["vp"].reshape(depth, 1, s, ATT_HEADS, V_DIM),
            st["hp"], st["cp"].reshape(depth, 1, CONV_W - 1, -1),
            st["ks"].reshape(depth, db, 1, ATT_HEADS, 2, QK_DIM),
            st["vs"].reshape(depth, db, 1, ATT_HEADS, V_DIM),
            st["hs"], st["cs"])
```
